```python
import math
import jax, jax.numpy as jnp
from jax import lax
import numpy as np

D_MODEL = 4096
BATCH = 2
SEQ = 8192
DEPTH = 2

N_MIXERS = 2
NORM_EPS = 1e-6
MLA_HEADS = 32
Q_LORA = 1024
KV_LORA = 512
NOPE_DIM = 128
ROPE_DIM = 64
V_DIM = 128
QK_DIM = NOPE_DIM + ROPE_DIM
ROPE_THETA = 10000.0
Q_BLOCK = 128
MLA_WIDTH = MLA_HEADS * V_DIM
MLA_IN = Q_LORA + KV_LORA + ROPE_DIM + MLA_WIDTH
HG_EXPAND = 128
HG_HEADS = D_MODEL // HG_EXPAND
HG_K = HG_EXPAND
HG_V = HG_EXPAND
HG_WIDTH = HG_HEADS * HG_V
HG_IN = 2 * HG_HEADS * HG_K + 2 * HG_WIDTH
CHUNK = 64

kernel_name = "mla_hgrn2_interleaved_gated_trunk"


def rmsnorm(x, w):
    xf = x.astype(jnp.float32)
    y = xf * lax.rsqrt(jnp.mean(xf * xf, axis=-1, keepdims=True) + NORM_EPS)
    return (y * w.astype(jnp.float32)).astype(x.dtype)


def rope(t, positions):
    d = t.shape[-1]
    inv_freq = 1.0 / (ROPE_THETA ** (jnp.arange(0, d, 2, dtype=jnp.float32) / d))
    ang = positions.astype(jnp.float32)[..., None] * inv_freq
    cos = jnp.cos(ang)[:, :, None, :]
    sin = jnp.sin(ang)[:, :, None, :]
    tf = t.astype(jnp.float32)
    t1, t2 = tf[..., : d // 2], tf[..., d // 2:]
    out = jnp.concatenate([t1 * cos - t2 * sin, t2 * cos + t1 * sin], axis=-1)
    return out.astype(t.dtype)


def causal_block_attention(q, k, v, scale):
    B, S, H, D = q.shape
    nb = S // Q_BLOCK
    qb = q.reshape(B, nb, Q_BLOCK, H, D).transpose(1, 0, 2, 3, 4)
    kpos = jnp.arange(S)
    neg = jnp.finfo(jnp.float32).min

    def one_block(args):
        qi, blk = args
        s = jnp.einsum('bqhd,bkhd->bhqk', qi, k).astype(jnp.float32) * scale
        qpos = blk * Q_BLOCK + jnp.arange(Q_BLOCK)
        mask = kpos[None, :] <= qpos[:, None]
        s = jnp.where(mask[None, None], s, neg)
        p = jax.nn.softmax(s, axis=-1).astype(v.dtype)
        return jnp.einsum('bhqk,bkhd->bqhd', p, v)

    out = lax.map(one_block, (qb, jnp.arange(nb)))
    return out.transpose(1, 0, 2, 3, 4).reshape(B, S, H, v.shape[-1])


def mla_mixer(h, positions, w_in, q_norm, w_uq, kv_norm, w_ukv, w_o):
    B, S, _ = h.shape
    proj = h @ w_in
    c_q, c_kv, k_rot, z = jnp.split(
        proj, [Q_LORA, Q_LORA + KV_LORA, Q_LORA + KV_LORA + ROPE_DIM], axis=-1)
    q = (rmsnorm(c_q, q_norm) @ w_uq).reshape(B, S, MLA_HEADS, QK_DIM)
    q_nope, q_rot = q[..., :NOPE_DIM], rope(q[..., NOPE_DIM:], positions)
    k_rot = rope(k_rot[:, :, None, :], positions)
    kv = (rmsnorm(c_kv, kv_norm) @ w_ukv).reshape(B, S, MLA_HEADS, NOPE_DIM + V_DIM)
    k_nope, v = kv[..., :NOPE_DIM], kv[..., NOPE_DIM:]
    q_full = jnp.concatenate([q_nope, q_rot], axis=-1)
    k_full = jnp.concatenate(
        [k_nope, jnp.broadcast_to(k_rot, (B, S, MLA_HEADS, ROPE_DIM))], axis=-1)
    o = causal_block_attention(q_full, k_full, v, 1.0 / math.sqrt(QK_DIM))
    o = o.reshape(B, S, MLA_WIDTH) * jax.nn.silu(z)
    return o @ w_o


def hgrn2_chunk_recurrence(q, k, v, log_f):
    B, S, H, K = q.shape
    V = v.shape[-1]
    nc = S // CHUNK

    def to_chunks(t):
        return t.astype(jnp.float32).reshape(B, nc, CHUNK, H, t.shape[-1]).transpose(1, 0, 3, 2, 4)

    qc, kc, vc, gc = to_chunks(q), to_chunks(k), to_chunks(v), to_chunks(log_f)
    causal = jnp.tril(jnp.ones((CHUNK, CHUNK), dtype=bool))

    def step(state, xs):
        qi, ki, vi, gi = xs
        b = jnp.cumsum(gi, axis=-2)
        b_ref = b[..., CHUNK // 2:CHUNK // 2 + 1, :]
        b_end = b[..., -1:, :]
        att = jnp.einsum('bhck,bhsk->bhcs', qi * jnp.exp(b - b_ref), ki * jnp.exp(b_ref - b))
        att = jnp.where(causal, att, 0.0)
        o = (jnp.einsum('bhcs,bhsv->bhcv', att, vi)
             + jnp.einsum('bhck,bhkv->bhcv', qi * jnp.exp(b), state))
        new_state = (state * jnp.exp(b_end)[..., 0, :, None]
                     + jnp.einsum('bhsk,bhsv->bhkv', ki * jnp.exp(b_end - b), vi))
        return new_state, o

    init = jnp.zeros((B, H, K, V), dtype=jnp.float32)
    _, o = lax.scan(step, init, (qc, kc, vc, gc))
    return o.transpose(1, 0, 3, 2, 4).reshape(B, S, H, V).astype(v.dtype)


def hgrn2_mixer(h, lower_bound, w_in, g_norm, w_o):
    B, S, _ = h.shape
    fd = HG_HEADS * HG_K
    proj = h @ w_in
    q, f, i, z = jnp.split(proj, [fd, 2 * fd, 2 * fd + HG_WIDTH], axis=-1)
    q = jax.nn.silu(q)
    lb = lower_bound.astype(jnp.float32)
    forget = lb + (1.0 - lb) * jax.nn.sigmoid(f.astype(jnp.float32))
    k = (1.0 - forget).astype(h.dtype)
    log_f = jnp.log(forget)
    rs = lambda t: t.reshape(B, S, HG_HEADS, t.shape[-1] // HG_HEADS)
    o = hgrn2_chunk_recurrence(rs(q), rs(k), rs(i), rs(log_f))
    o = rmsnorm(o, g_norm).reshape(B, S, HG_WIDTH) * jax.nn.silu(z)
    return o @ w_o


def setup_inputs(seed: int = 0) -> dict:
    key = jax.random.key(seed)
    ks = jax.random.split(key, 20)
    nrm = lambda k, shape, fan_in: jax.random.normal(k, shape, jnp.float32) * (fan_in ** -0.5)
    gain = lambda k, n: 1.0 + 0.02 * jax.random.normal(k, (n,), jnp.float32)
    x = jax.random.normal(ks[0], (BATCH, SEQ, D_MODEL), jnp.float32)
    offsets = jax.random.randint(ks[1], (BATCH, 1), 0, 1024, dtype=jnp.int32)
    positions = (offsets + jnp.arange(SEQ, dtype=jnp.int32)[None, :]).astype(jnp.int32)
    return {
        "x": x,
        "positions": positions,
        "l0_norm": gain(ks[2], D_MODEL),
        "l0_w_in": nrm(ks[3], (D_MODEL, MLA_IN), D_MODEL),
        "l0_q_norm": gain(ks[4], Q_LORA),
        "l0_w_uq": nrm(ks[5], (Q_LORA, MLA_HEADS * QK_DIM), Q_LORA),
        "l0_kv_norm": gain(ks[6], KV_LORA),
        "l0_w_ukv": nrm(ks[7], (KV_LORA, MLA_HEADS * (NOPE_DIM + V_DIM)), KV_LORA),
        "l0_w_o": nrm(ks[8], (MLA_WIDTH, D_MODEL), MLA_WIDTH),
        "l1_norm": gain(ks[9], D_MODEL),
        "l1_w_in": nrm(ks[10], (D_MODEL, HG_IN), D_MODEL),
        "l1_g_norm": gain(ks[11], HG_V),
        "l1_w_o": nrm(ks[12], (HG_WIDTH, D_MODEL), HG_WIDTH),
        "lower_bounds": 0.1 * jax.random.normal(ks[13], (DEPTH, HG_HEADS * HG_K), jnp.float32),
        "final_norm": gain(ks[14], D_MODEL),
    }


def reference(x, positions, l0_norm, l0_w_in, l0_q_norm, l0_w_uq, l0_kv_norm, l0_w_ukv, l0_w_o,
              l1_norm, l1_w_in, l1_g_norm, l1_w_o, lower_bounds, final_norm):
    lb_soft = jax.nn.softmax(lower_bounds.astype(jnp.float32), axis=0)
    lb_all = jnp.cumsum(lb_soft, axis=0) - lb_soft[0]
    layers = [
        (l0_norm, (l0_w_in, l0_q_norm, l0_w_uq, l0_kv_norm, l0_w_ukv, l0_w_o)),
        (l1_norm, (l1_w_in, l1_g_norm, l1_w_o)),
    ]
    for i in range(DEPTH):
        norm_w, params = layers[i]
        h = rmsnorm(x, norm_w)
        if i % N_MIXERS == 0:
            delta = mla_mixer(h, positions, *params)
        else:
            delta = hgrn2_mixer(h, lb_all[i], *params)
        x = x + delta.astype(x.dtype)
    return rmsnorm(x, final_norm)
```

```python
import functools
import math

import jax
import jax.numpy as jnp
from jax import lax
from jax.experimental import pallas as pl
from jax.experimental.pallas import tpu as pltpu

NORM_EPS = 1e-6
NOPE_DIM = 128
ROPE_DIM = 64
V_DIM = 128
QK_DIM = NOPE_DIM + ROPE_DIM
QK_PAD = 256
ROPE_THETA = 10000.0
HG_DIM = 128
CHUNK = 64
LANES = 128
VMEM_LIMIT = 52 * 1024 * 1024
LOG2E = 1.4426950408889634

F32 = jnp.float32
BF16 = jnp.bfloat16


def _params(n_axes, vmem=VMEM_LIMIT):
    return pltpu.CompilerParams(
        dimension_semantics=("arbitrary",) * n_axes, vmem_limit_bytes=vmem)


def _sigmoid(x):
    return 1.0 / (1.0 + jnp.exp(-x))


def _silu(x):
    return x * _sigmoid(x)


def _pick(n, pref):
    t = min(n, pref)
    while n % t:
        t //= 2
    return t


def _rmsnorm_kernel(x_ref, w_ref, o_ref):
    x = x_ref[...]
    r = lax.rsqrt(jnp.mean(x * x, axis=-1, keepdims=True) + NORM_EPS)
    o_ref[...] = (x * r * w_ref[...]).astype(o_ref.dtype)


def _rmsnorm(x, w, out_dtype, name):
    m, d = x.shape
    tm = _pick(m, 256)
    return pl.pallas_call(
        _rmsnorm_kernel,
        grid=(m // tm,),
        in_specs=[pl.BlockSpec((tm, d), lambda i: (i, 0)),
                  pl.BlockSpec((1, d), lambda i: (0, 0))],
        out_specs=pl.BlockSpec((tm, d), lambda i: (i, 0)),
        out_shape=jax.ShapeDtypeStruct((m, d), out_dtype),
        compiler_params=_params(1),
        name=name,
    )(x, w.reshape(1, d))


def _mm_kernel(a_ref, b_ref, *refs, epilogue, n_extra):
    acc = jnp.dot(a_ref[...], b_ref[...], preferred_element_type=F32)
    epilogue(acc, refs[:n_extra], refs[n_extra:])


def _matmul(a, b, *, tm, tn, epilogue, extras=(), extra_specs=(), out_shape, out_specs, name):
    m, k = a.shape
    _, n = b.shape
    tm = _pick(m, tm)
    tn = _pick(n, tn)
    kern = functools.partial(_mm_kernel, epilogue=epilogue, n_extra=len(extras))
    return pl.pallas_call(
        kern,
        grid=(m // tm, n // tn),
        in_specs=[pl.BlockSpec((tm, k), lambda i, j: (i, 0)),
                  pl.BlockSpec((k, tn), lambda i, j: (0, j))] + list(extra_specs),
        out_specs=out_specs,
        out_shape=out_shape,
        compiler_params=_params(2),
        name=name,
    )(a, b, *extras)


def _ep_silu(acc, extras, outs):
    outs[0][...] = _silu(acc).astype(outs[0].dtype)


def _ep_cast(acc, extras, outs):
    outs[0][...] = acc.astype(outs[0].dtype)


def _ep_residual(acc, extras, outs):
    outs[0][...] = extras[0][...] + acc


def _simple_mm(a, b, epilogue, out_dtype, name, tm=1024, tn=512, residual=None):
    m, _ = a.shape
    _, n = b.shape
    tm = _pick(m, tm)
    tn = _pick(n, tn)
    tile = pl.BlockSpec((tm, tn), lambda i, j: (i, j))
    extras, extra_specs = ((residual,), (tile,)) if residual is not None else ((), ())
    return _matmul(a, b, tm=tm, tn=tn, epilogue=epilogue, extras=extras, extra_specs=extra_specs,
                   out_shape=jax.ShapeDtypeStruct((m, n), out_dtype), out_specs=tile, name=name)


def _rope_pair(x, cos2, sin2):
    return x * cos2 + pltpu.roll(x, ROPE_DIM, 1) * sin2


def _mla_down_kernel(a_ref, b_ref, qn_ref, kvn_ref, cos_ref, sin_ref,
                     cq_ref, ckv_ref, kr_ref, acc_ref, *, q_lora, kv_lora):
    kk = pl.program_id(1)
    part = jnp.dot(a_ref[...], b_ref[...], preferred_element_type=F32)

    @pl.when(kk == 0)
    def _():
        acc_ref[...] = part

    @pl.when(kk != 0)
    def _():
        acc_ref[...] += part

    @pl.when(kk == pl.num_programs(1) - 1)
    def _():
        cq = acc_ref[:, :q_lora]
        r = lax.rsqrt(jnp.mean(cq * cq, axis=-1, keepdims=True) + NORM_EPS)
        cq_ref[...] = (cq * r * qn_ref[...]).astype(cq_ref.dtype)
        ckv = acc_ref[:, q_lora:q_lora + kv_lora]
        r = lax.rsqrt(jnp.mean(ckv * ckv, axis=-1, keepdims=True) + NORM_EPS)
        ckv_ref[...] = (ckv * r * kvn_ref[...]).astype(ckv_ref.dtype)
        kr = acc_ref[:, q_lora + kv_lora:]
        kr_ref[...] = _rope_pair(kr, cos_ref[...], sin_ref[...]).astype(kr_ref.dtype)


def _mla_down(h, w_small, q_norm, kv_norm, cos2, sin2, q_lora, kv_lora):
    m, d = h.shape
    n = w_small.shape[1]
    tm = _pick(m, 1024)
    tk = _pick(d, 1024)
    kern = functools.partial(_mla_down_kernel, q_lora=q_lora, kv_lora=kv_lora)
    row = lambda i, k: (i, 0)
    return pl.pallas_call(
        kern,
        grid=(m // tm, d // tk),
        in_specs=[pl.BlockSpec((tm, tk), lambda i, k: (i, k)),
                  pl.BlockSpec((tk, n), lambda i, k: (k, 0)),
                  pl.BlockSpec((1, q_lora), lambda i, k: (0, 0)),
                  pl.BlockSpec((1, kv_lora), lambda i, k: (0, 0)),
                  pl.BlockSpec((tm, LANES), row),
                  pl.BlockSpec((tm, LANES), row)],
        out_specs=[pl.BlockSpec((tm, q_lora), row),
                   pl.BlockSpec((tm, kv_lora), row),
                   pl.BlockSpec((tm, LANES), row)],
        out_shape=[jax.ShapeDtypeStruct((m, q_lora), BF16),
                   jax.ShapeDtypeStruct((m, kv_lora), BF16),
                   jax.ShapeDtypeStruct((m, LANES), BF16)],
        scratch_shapes=[pltpu.VMEM((tm, n), F32)],
        compiler_params=_params(2),
        name="mla_down",
    )(h, w_small, q_norm.reshape(1, -1), kv_norm.reshape(1, -1), cos2, sin2)


def _ep_q_up(acc, extras, outs, *, heads, scale):
    cos2 = extras[0][...]
    sin2 = extras[1][...]
    for j in range(heads):
        base = j * QK_PAD
        outs[0][:, base:base + NOPE_DIM] = (acc[:, base:base + NOPE_DIM] * scale).astype(BF16)
        rot = _rope_pair(acc[:, base + NOPE_DIM:base + QK_PAD], cos2, sin2)
        outs[0][:, base + NOPE_DIM:base + QK_PAD] = (rot * scale).astype(BF16)


def _ep_kv_up(acc, extras, outs, *, heads):
    kr = extras[0][...]
    for j in range(heads):
        base = j * QK_PAD
        outs[0][:, base:base + NOPE_DIM] = acc[:, base:base + NOPE_DIM].astype(BF16)
        outs[0][:, base + NOPE_DIM:base + QK_PAD] = kr
        outs[1][:, j * V_DIM:(j + 1) * V_DIM] = acc[:, base + NOPE_DIM:base + QK_PAD].astype(BF16)


def _attn_kernel(q_ref, k_ref, v_ref, z_ref, o_ref, *, tq):
    qi = pl.program_id(2)
    q = q_ref[...]
    neg = jnp.finfo(F32).min

    def step(kj, carry, masked):
        m, l, acc = carry
        off = pl.multiple_of(kj * tq, tq)
        k = k_ref[pl.ds(off, tq), :]
        v = v_ref[pl.ds(off, tq), :]
        s = lax.dot_general(q, k, (((1,), (1,)), ((), ())), preferred_element_type=F32)
        if masked:
            row = lax.broadcasted_iota(jnp.int32, (tq, tq), 0)
            col = lax.broadcasted_iota(jnp.int32, (tq, tq), 1)
            s = jnp.where(col <= row, s, neg)
        m_new = jnp.maximum(m, jnp.max(s, axis=1, keepdims=True))
        alpha = jnp.exp2(m - m_new)
        p = jnp.exp2(s - m_new)
        l = alpha * l + jnp.sum(p, axis=1, keepdims=True)
        acc = alpha * acc + jnp.dot(p.astype(BF16), v, preferred_element_type=F32)
        return m_new, l, acc

    init = (jnp.full((tq, 1), neg, F32), jnp.zeros((tq, 1), F32), jnp.zeros((tq, V_DIM), F32))
    carry = lax.fori_loop(0, qi, functools.partial(step, masked=False), init)
    _, l, acc = step(qi, carry, True)
    o_ref[...] = ((acc / l) * z_ref[...].astype(F32)).astype(o_ref.dtype)


def _attention(q, k, v, zg, heads):
    b, s, _ = q.shape
    tq = _pick(s, 512)
    kern = functools.partial(_attn_kernel, tq=tq)
    return pl.pallas_call(
        kern,
        grid=(b, heads, s // tq),
        in_specs=[pl.BlockSpec((None, tq, QK_PAD), lambda bi, h, i: (bi, i, h)),
                  pl.BlockSpec((None, s, QK_PAD), lambda bi, h, i: (bi, 0, h)),
                  pl.BlockSpec((None, s, V_DIM), lambda bi, h, i: (bi, 0, h)),
                  pl.BlockSpec((None, tq, V_DIM), lambda bi, h, i: (bi, i, h))],
        out_specs=pl.BlockSpec((None, tq, V_DIM), lambda bi, h, i: (bi, i, h)),
        out_shape=jax.ShapeDtypeStruct((b, s, heads * V_DIM), BF16),
        compiler_params=_params(3),
        name="mla_attention",
    )(q, k, v, zg)


def _cumsum_rows(x):
    n = x.shape[0]
    row = lax.broadcasted_iota(jnp.int32, x.shape, 0)
    sh = 1
    while sh < n:
        x = x + jnp.where(row >= sh, pltpu.roll(x, sh, 0), 0.0)
        sh *= 2
    return x


def _hgrn_kernel(q_ref, f_ref, i_ref, z_ref, lb_ref, gn_ref, o_ref, st_ref, *, tt, heads, layer):
    @pl.when(pl.program_id(2) == 0)
    def _():
        st_ref[...] = jnp.zeros_like(st_ref)

    lbp = lb_ref[...]
    e = jnp.exp(lbp - jnp.max(lbp, axis=0, keepdims=True))
    soft = e / jnp.sum(e, axis=0, keepdims=True)
    lb_all = jnp.sum(soft[:layer + 1], axis=0, keepdims=True) - soft[0:1]

    row = lax.broadcasted_iota(jnp.int32, (CHUNK, CHUNK), 0)
    col = lax.broadcasted_iota(jnp.int32, (CHUNK, CHUNK), 1)
    causal = col <= row
    gn = gn_ref[...]

    for hh in range(heads):
        cs = slice(hh * HG_DIM, (hh + 1) * HG_DIM)
        lb = lb_all[:, cs]
        for c in range(tt // CHUNK):
            rs = slice(c * CHUNK, (c + 1) * CHUNK)
            q = q_ref[rs, cs].astype(F32)
            v = i_ref[rs, cs]
            forget = lb + (1.0 - lb) * _sigmoid(f_ref[rs, cs])
            k = 1.0 - forget
            b = _cumsum_rows(jnp.log(forget))
            b_mid = b[CHUNK // 2:CHUNK // 2 + 1, :]
            b_end = b[CHUNK - 1:CHUNK, :]
            qd = (q * jnp.exp(b - b_mid)).astype(BF16)
            kd = (k * jnp.exp(b_mid - b)).astype(BF16)
            att = lax.dot_general(qd, kd, (((1,), (1,)), ((), ())), preferred_element_type=F32)
            att = jnp.where(causal, att, 0.0).astype(BF16)
            st = st_ref[hh]
            qs = (q * jnp.exp(b)).astype(BF16)
            o = (jnp.dot(att, v, preferred_element_type=F32)
                 + lax.dot_general(qs, st.astype(BF16), (((1,), (1,)), ((), ())),
                                   preferred_element_type=F32))
            ks = (k * jnp.exp(b_end - b)).astype(BF16)
            st_ref[hh] = st * jnp.exp(b_end) + lax.dot_general(
                v, ks, (((0,), (0,)), ((), ())), preferred_element_type=F32)
            r = lax.rsqrt(jnp.mean(o * o, axis=-1, keepdims=True) + NORM_EPS)
            o_ref[rs, cs] = (o * r * gn * z_ref[rs, cs].astype(F32)).astype(o_ref.dtype)


def _hgrn_recurrence(qs, f, iv, zs, lower_bounds, g_norm, layer):
    b, s, w = qs.shape
    heads = 2
    cols = heads * HG_DIM
    tt = _pick(s, 256)
    depth = lower_bounds.shape[0]
    kern = functools.partial(_hgrn_kernel, tt=tt, heads=heads, layer=layer)
    blk = pl.BlockSpec((None, tt, cols), lambda bi, h, t: (bi, t, h))
    return pl.pallas_call(
        kern,
        grid=(b, w // cols, s // tt),
        in_specs=[blk, blk, blk, blk,
                  pl.BlockSpec((depth, cols), lambda bi, h, t: (0, h)),
                  pl.BlockSpec((1, HG_DIM), lambda bi, h, t: (0, 0))],
        out_specs=blk,
        out_shape=jax.ShapeDtypeStruct((b, s, w), BF16),
        scratch_shapes=[pltpu.VMEM((heads, HG_DIM, HG_DIM), F32)],
        compiler_params=_params(3),
        name="hgrn_recurrence",
    )(qs, f, iv, zs, lower_bounds, g_norm.reshape(1, HG_DIM))


def _rot_half_cols(w):
    half = w.shape[-1] // 2
    return jnp.concatenate([-w[..., half:], w[..., :half]], axis=-1)


def kernel(x, positions, l0_norm, l0_w_in, l0_q_norm, l0_w_uq, l0_kv_norm, l0_w_ukv, l0_w_o,
           l1_norm, l1_w_in, l1_g_norm, l1_w_o, lower_bounds, final_norm):
    bsz, seq, d = x.shape
    m = bsz * seq
    q_lora = l0_q_norm.shape[0]
    kv_lora = l0_kv_norm.shape[0]
    heads = l0_w_uq.shape[1] // QK_DIM
    width = heads * V_DIM
    x2 = x.reshape(m, d)

    inv_freq = 1.0 / (ROPE_THETA ** (jnp.arange(0, ROPE_DIM, 2, dtype=F32) / ROPE_DIM))
    ang = positions.reshape(m, 1).astype(F32) * inv_freq
    zeros = jnp.zeros((m, ROPE_DIM), F32)
    cos2 = jnp.concatenate([jnp.cos(ang), jnp.cos(ang), zeros], axis=-1)
    sin2 = jnp.concatenate([jnp.sin(ang), jnp.sin(ang), zeros], axis=-1)

    w_kr = l0_w_in[:, q_lora + kv_lora:q_lora + kv_lora + ROPE_DIM]
    w_small = jnp.concatenate(
        [l0_w_in[:, :q_lora + kv_lora], w_kr, _rot_half_cols(w_kr)], axis=-1).astype(BF16)
    w_z0 = l0_w_in[:, q_lora + kv_lora + ROPE_DIM:].astype(BF16)
    wq = l0_w_uq.reshape(q_lora, heads, QK_DIM)
    wq_rot = wq[..., NOPE_DIM:]
    w_q = jnp.concatenate([wq, _rot_half_cols(wq_rot)], axis=-1).reshape(
        q_lora, heads * QK_PAD).astype(BF16)
    w_kv = l0_w_ukv.astype(BF16)

    h0 = _rmsnorm(x2, l0_norm, BF16, "norm0")
    cq, ckv, kr = _mla_down(h0, w_small, l0_q_norm, l0_kv_norm, cos2, sin2, q_lora, kv_lora)
    zg0 = _simple_mm(h0, w_z0, _ep_silu, BF16, "mla_gate")

    hq = 4
    tm = _pick(m, 1024)
    row_tab = pl.BlockSpec((tm, LANES), lambda i, j: (i, 0))
    scale = LOG2E / math.sqrt(QK_DIM)
    q = _matmul(cq, w_q, tm=tm, tn=hq * QK_PAD,
                epilogue=functools.partial(_ep_q_up, heads=hq, scale=scale),
                extras=(cos2, sin2), extra_specs=(row_tab, row_tab),
                out_shape=jax.ShapeDtypeStruct((m, heads * QK_PAD), BF16),
                out_specs=pl.BlockSpec((tm, hq * QK_PAD), lambda i, j: (i, j)), name="mla_q_up")
    kfull, v = _matmul(ckv, w_kv, tm=tm, tn=hq * QK_PAD,
                       epilogue=functools.partial(_ep_kv_up, heads=hq),
                       extras=(kr,), extra_specs=(row_tab,),
                       out_shape=[jax.ShapeDtypeStruct((m, heads * QK_PAD), BF16),
                                  jax.ShapeDtypeStruct((m, width), BF16)],
                       out_specs=[pl.BlockSpec((tm, hq * QK_PAD), lambda i, j: (i, j)),
                                  pl.BlockSpec((tm, hq * V_DIM), lambda i, j: (i, j))],
                       name="mla_kv_up")
    og0 = _attention(q.reshape(bsz, seq, -1), kfull.reshape(bsz, seq, -1),
                     v.reshape(bsz, seq, -1), zg0.reshape(bsz, seq, -1), heads)
    x1 = _simple_mm(og0.reshape(m, width), l0_w_o.astype(BF16), _ep_residual, F32, "mla_out",
                    residual=x2)

    hw = l1_w_o.shape[0]
    w1 = l1_w_in.astype(BF16)
    h1 = _rmsnorm(x1, l1_norm, BF16, "norm1")
    qs = _simple_mm(h1, w1[:, :hw], _ep_silu, BF16, "hgrn_q")
    fr = _simple_mm(h1, w1[:, hw:2 * hw], _ep_cast, F32, "hgrn_f")
    iv = _simple_mm(h1, w1[:, 2 * hw:3 * hw], _ep_cast, BF16, "hgrn_i")
    zs = _simple_mm(h1, w1[:, 3 * hw:], _ep_silu, BF16, "hgrn_z")
    r3 = lambda t: t.reshape(bsz, seq, hw)
    og1 = _hgrn_recurrence(r3(qs), r3(fr), r3(iv), r3(zs), lower_bounds, l1_g_norm, layer=1)
    x2f = _simple_mm(og1.reshape(m, hw), l1_w_o.astype(BF16), _ep_residual, F32, "hgrn_out",
                     residual=x1)

    return _rmsnorm(x2f, final_norm, F32, "final_norm").reshape(bsz, seq, d)
```

```python
import functools
import math

import jax
import jax.numpy as jnp
from jax import lax
from jax.experimental import pallas as pl
from jax.experimental.pallas import tpu as pltpu

NORM_EPS = 1e-6
NOPE_DIM = 128
ROPE_DIM = 64
V_DIM = 128
QK_DIM = NOPE_DIM + ROPE_DIM
QK_PAD = 256
ROPE_THETA = 10000.0
HG_DIM = 128
CHUNK = 64
LANES = 128
VMEM_LIMIT = 52 * 1024 * 1024
LOG2E = 1.4426950408889634

F32 = jnp.float32
BF16 = jnp.bfloat16


def _params(n_axes, vmem=VMEM_LIMIT):
    return pltpu.CompilerParams(
        dimension_semantics=("arbitrary",) * n_axes, vmem_limit_bytes=vmem)


def _sigmoid(x):
    return 1.0 / (1.0 + jnp.exp(-x))


def _silu(x):
    return x * _sigmoid(x)


def _pick(n, pref):
    t = min(n, pref)
    while n % t:
        t //= 2
    return t


def _rmsnorm_kernel(x_ref, w_ref, o_ref):
    x = x_ref[...]
    r = lax.rsqrt(jnp.mean(x * x, axis=-1, keepdims=True) + NORM_EPS)
    o_ref[...] = (x * r * w_ref[...]).astype(o_ref.dtype)


def _rmsnorm(x, w, out_dtype, name):
    m, d = x.shape
    tm = _pick(m, 256)
    return pl.pallas_call(
        _rmsnorm_kernel,
        grid=(m // tm,),
        in_specs=[pl.BlockSpec((tm, d), lambda i: (i, 0)),
                  pl.BlockSpec((1, d), lambda i: (0, 0))],
        out_specs=pl.BlockSpec((tm, d), lambda i: (i, 0)),
        out_shape=jax.ShapeDtypeStruct((m, d), out_dtype),
        compiler_params=_params(1),
        name=name,
    )(x, w.reshape(1, d))


def _mm_kernel(a_ref, b_ref, *refs, epilogue, n_extra):
    acc = jnp.dot(a_ref[...], b_ref[...], preferred_element_type=F32)
    epilogue(acc, refs[:n_extra], refs[n_extra:])


def _matmul(a, b, *, tm, tn, epilogue, extras=(), extra_specs=(), out_shape, out_specs, name):
    m, k = a.shape
    _, n = b.shape
    tm = _pick(m, tm)
    tn = _pick(n, tn)
    kern = functools.partial(_mm_kernel, epilogue=epilogue, n_extra=len(extras))
    return pl.pallas_call(
        kern,
        grid=(m // tm, n // tn),
        in_specs=[pl.BlockSpec((tm, k), lambda i, j: (i, 0)),
                  pl.BlockSpec((k, tn), lambda i, j: (0, j))] + list(extra_specs),
        out_specs=out_specs,
        out_shape=out_shape,
        compiler_params=_params(2),
        name=name,
    )(a, b, *extras)


def _ep_silu(acc, extras, outs):
    outs[0][...] = _silu(acc).astype(outs[0].dtype)


def _ep_cast(acc, extras, outs):
    outs[0][...] = acc.astype(outs[0].dtype)


def _ep_residual(acc, extras, outs):
    outs[0][...] = extras[0][...] + acc


def _simple_mm(a, b, epilogue, out_dtype, name, tm=1024, tn=512, residual=None):
    m, _ = a.shape
    _, n = b.shape
    tm = _pick(m, tm)
    tn = _pick(n, tn)
    tile = pl.BlockSpec((tm, tn), lambda i, j: (i, j))
    extras, extra_specs = ((residual,), (tile,)) if residual is not None else ((), ())
    return _matmul(a, b, tm=tm, tn=tn, epilogue=epilogue, extras=extras, extra_specs=extra_specs,
                   out_shape=jax.ShapeDtypeStruct((m, n), out_dtype), out_specs=tile, name=name)


def _rope_pair(x, cos2, sin2):
    return x * cos2 + pltpu.roll(x, ROPE_DIM, 1) * sin2


def _mla_down_kernel(a_ref, b_ref, qn_ref, kvn_ref, cos_ref, sin_ref,
                     cq_ref, ckv_ref, kr_ref, acc_ref, *, q_lora, kv_lora):
    kk = pl.program_id(1)
    part = jnp.dot(a_ref[...], b_ref[...], preferred_element_type=F32)

    @pl.when(kk == 0)
    def _():
        acc_ref[...] = part

    @pl.when(kk != 0)
    def _():
        acc_ref[...] += part

    @pl.when(kk == pl.num_programs(1) - 1)
    def _():
        cq = acc_ref[:, :q_lora]
        r = lax.rsqrt(jnp.mean(cq * cq, axis=-1, keepdims=True) + NORM_EPS)
        cq_ref[...] = (cq * r * qn_ref[...]).astype(cq_ref.dtype)
        ckv = acc_ref[:, q_lora:q_lora + kv_lora]
        r = lax.rsqrt(jnp.mean(ckv * ckv, axis=-1, keepdims=True) + NORM_EPS)
        ckv_ref[...] = (ckv * r * kvn_ref[...]).astype(ckv_ref.dtype)
        kr = acc_ref[:, q_lora + kv_lora:]
        kr_ref[...] = _rope_pair(kr, cos_ref[...], sin_ref[...]).astype(kr_ref.dtype)


def _mla_down(h, w_small, q_norm, kv_norm, cos2, sin2, q_lora, kv_lora):
    m, d = h.shape
    n = w_small.shape[1]
    tm = _pick(m, 1024)
    tk = _pick(d, 1024)
    kern = functools.partial(_mla_down_kernel, q_lora=q_lora, kv_lora=kv_lora)
    row = lambda i, k: (i, 0)
    return pl.pallas_call(
        kern,
        grid=(m // tm, d // tk),
        in_specs=[pl.BlockSpec((tm, tk), lambda i, k: (i, k)),
                  pl.BlockSpec((tk, n), lambda i, k: (k, 0)),
                  pl.BlockSpec((1, q_lora), lambda i, k: (0, 0)),
                  pl.BlockSpec((1, kv_lora), lambda i, k: (0, 0)),
                  pl.BlockSpec((tm, LANES), row),
                  pl.BlockSpec((tm, LANES), row)],
        out_specs=[pl.BlockSpec((tm, q_lora), row),
                   pl.BlockSpec((tm, kv_lora), row),
                   pl.BlockSpec((tm, LANES), row)],
        out_shape=[jax.ShapeDtypeStruct((m, q_lora), BF16),
                   jax.ShapeDtypeStruct((m, kv_lora), BF16),
                   jax.ShapeDtypeStruct((m, LANES), BF16)],
        scratch_shapes=[pltpu.VMEM((tm, n), F32)],
        compiler_params=_params(2),
        name="mla_down",
    )(h, w_small, q_norm.reshape(1, -1), kv_norm.reshape(1, -1), cos2, sin2)


def _ep_q_up(acc, extras, outs, *, heads, scale):
    cos2 = extras[0][...]
    sin2 = extras[1][...]
    for j in range(heads):
        base = j * QK_PAD
        outs[0][:, base:base + NOPE_DIM] = (acc[:, base:base + NOPE_DIM] * scale).astype(BF16)
        rot = _rope_pair(acc[:, base + NOPE_DIM:base + QK_PAD], cos2, sin2)
        outs[0][:, base + NOPE_DIM:base + QK_PAD] = (rot * scale).astype(BF16)


def _ep_kv_up(acc, extras, outs, *, heads):
    kr = extras[0][...]
    for j in range(heads):
        base = j * QK_PAD
        outs[0][:, base:base + NOPE_DIM] = acc[:, base:base + NOPE_DIM].astype(BF16)
        outs[0][:, base + NOPE_DIM:base + QK_PAD] = kr
        outs[1][:, j * V_DIM:(j + 1) * V_DIM] = acc[:, base + NOPE_DIM:base + QK_PAD].astype(BF16)


def _attn_kernel(q_ref, k_ref, v_ref, z_ref, o_ref, vx_ref, sa_ref, sb_ref, pa_ref, pb_ref,
                 aa_ref, ab_ref, m_ref, acc_ref, *, tq, tk):
    qi = pl.program_id(2)
    neg = jnp.finfo(F32).min

    @pl.when(qi == 0)
    def _():
        vx_ref[:, :V_DIM] = v_ref[...]
        vx_ref[:, V_DIM:] = jnp.ones((vx_ref.shape[0], V_DIM), BF16)

    upper = slice(0, tk)
    lower = slice(tk, tq)
    every = slice(None)

    def scores(s_ref, kt, rows=every):
        off = pl.multiple_of(kt * tk, tk)
        s_ref[rows, :] = lax.dot_general(q_ref[rows, :], k_ref[pl.ds(off, tk), :],
                                         (((1,), (1,)), ((), ())), preferred_element_type=F32)

    def softmax(s_ref, p_ref, a_ref, rows=every, masked=False, first=False):
        s = s_ref[rows, :]
        if masked:
            row = lax.broadcasted_iota(jnp.int32, s.shape, 0)
            col = lax.broadcasted_iota(jnp.int32, s.shape, 1)
            s = jnp.where(col <= row, s, neg)
        mx = jnp.max(s, axis=1, keepdims=True)
        if first:
            m_new = jnp.broadcast_to(mx, (s.shape[0], LANES))
        else:
            m_old = m_ref[rows, :]
            m_new = jnp.maximum(m_old, mx)
            a_ref[rows, :] = jnp.exp2(m_old - m_new)
        m_ref[rows, :] = m_new
        p_ref[rows, :] = jnp.exp2(s - jnp.concatenate([m_new] * (tk // LANES), axis=1)).astype(BF16)

    def accumulate(p_ref, a_ref, vt, rows=every, first=False):
        off = pl.multiple_of(vt * tk, tk)
        pv = jnp.dot(p_ref[rows, :], vx_ref[pl.ds(off, tk), :], preferred_element_type=F32)
        if first:
            acc_ref[rows, :] = pv
        else:
            a = a_ref[rows, :]
            acc_ref[rows, :] = jnp.concatenate([a, a], axis=1) * acc_ref[rows, :] + pv

    d0 = 2 * qi
    scores(sb_ref, d0 + 1, lower)
    scores(sa_ref, d0)
    softmax(sb_ref, pb_ref, ab_ref, lower, masked=True, first=True)
    accumulate(pb_ref, ab_ref, d0 + 1, lower, first=True)
    scores(sb_ref, 0)
    acc_ref[upper, :] = jnp.zeros((tk, 2 * V_DIM), F32)
    aa_ref[upper, :] = jnp.zeros((tk, LANES), F32)
    softmax(sa_ref, pa_ref, aa_ref, upper, masked=True, first=True)
    softmax(sa_ref, pa_ref, aa_ref, lower)

    def body(i, carry):
        scores(sa_ref, 2 * i + 1)
        accumulate(pa_ref, aa_ref, jnp.where(i == 0, d0, 2 * i - 1))
        softmax(sb_ref, pb_ref, ab_ref)
        scores(sb_ref, 2 * i + 2)
        accumulate(pb_ref, ab_ref, 2 * i)
        softmax(sa_ref, pa_ref, aa_ref)
        return carry

    lax.fori_loop(0, qi, body, 0)
    accumulate(pa_ref, aa_ref, jnp.where(qi == 0, d0, 2 * qi - 1))
    acc = acc_ref[...]
    o = acc[:, :V_DIM] / acc[:, V_DIM:]
    o_ref[...] = (o * z_ref[...].astype(F32)).astype(o_ref.dtype)


def _attention(q, k, v, zg, heads):
    b, s, _ = q.shape
    tq = _pick(s, 1024)
    tk = tq // 2
    kern = functools.partial(_attn_kernel, tq=tq, tk=tk)
    return pl.pallas_call(
        kern,
        grid=(b, heads, s // tq),
        in_specs=[pl.BlockSpec((None, tq, QK_PAD), lambda bi, h, i: (bi, i, h)),
                  pl.BlockSpec((None, s, QK_PAD), lambda bi, h, i: (bi, 0, h)),
                  pl.BlockSpec((None, s, V_DIM), lambda bi, h, i: (bi, 0, h)),
                  pl.BlockSpec((None, tq, V_DIM), lambda bi, h, i: (bi, i, h))],
        out_specs=pl.BlockSpec((None, tq, V_DIM), lambda bi, h, i: (bi, i, h)),
        out_shape=jax.ShapeDtypeStruct((b, s, heads * V_DIM), BF16),
        scratch_shapes=[pltpu.VMEM((s, 2 * V_DIM), BF16),
                        pltpu.VMEM((tq, tk), F32), pltpu.VMEM((tq, tk), F32),
                        pltpu.VMEM((tq, tk), BF16), pltpu.VMEM((tq, tk), BF16),
                        pltpu.VMEM((tq, LANES), F32), pltpu.VMEM((tq, LANES), F32),
                        pltpu.VMEM((tq, LANES), F32), pltpu.VMEM((tq, 2 * V_DIM), F32)],
        compiler_params=_params(3),
        name="mla_attention",
    )(q, k, v, zg)


def _cumsum_rows(x):
    n = x.shape[0]
    row = lax.broadcasted_iota(jnp.int32, x.shape, 0)
    sh = 1
    while sh < n:
        x = x + jnp.where(row >= sh, pltpu.roll(x, sh, 0), 0.0)
        sh *= 2
    return x


def _hgrn_kernel(q_ref, f_ref, i_ref, z_ref, lb_ref, gn_ref, o_ref, st_ref, *, tt, heads, layer):
    @pl.when(pl.program_id(2) == 0)
    def _():
        st_ref[...] = jnp.zeros_like(st_ref)

    lbp = lb_ref[...]
    e = jnp.exp(lbp - jnp.max(lbp, axis=0, keepdims=True))
    soft = e / jnp.sum(e, axis=0, keepdims=True)
    lb_all = jnp.sum(soft[:layer + 1], axis=0, keepdims=True) - soft[0:1]

    row = lax.broadcasted_iota(jnp.int32, (CHUNK, CHUNK), 0)
    col = lax.broadcasted_iota(jnp.int32, (CHUNK, CHUNK), 1)
    causal = col <= row
    gn = gn_ref[...]

    for hh in range(heads):
        cs = slice(hh * HG_DIM, (hh + 1) * HG_DIM)
        lb = lb_all[:, cs]
        for c in range(tt // CHUNK):
            rs = slice(c * CHUNK, (c + 1) * CHUNK)
            q = q_ref[rs, cs].astype(F32)
            v = i_ref[rs, cs]
            forget = lb + (1.0 - lb) * _sigmoid(f_ref[rs, cs])
            k = 1.0 - forget
            b = _cumsum_rows(jnp.log(forget))
            b_mid = b[CHUNK // 2:CHUNK // 2 + 1, :]
            b_end = b[CHUNK - 1:CHUNK, :]
            qd = (q * jnp.exp(b - b_mid)).astype(BF16)
            kd = (k * jnp.exp(b_mid - b)).astype(BF16)
            att = lax.dot_general(qd, kd, (((1,), (1,)), ((), ())), preferred_element_type=F32)
            att = jnp.where(causal, att, 0.0).astype(BF16)
            st = st_ref[hh]
            qs = (q * jnp.exp(b)).astype(BF16)
            o = (jnp.dot(att, v, preferred_element_type=F32)
                 + lax.dot_general(qs, st.astype(BF16), (((1,), (1,)), ((), ())),
                                   preferred_element_type=F32))
            ks = (k * jnp.exp(b_end - b)).astype(BF16)
            st_ref[hh] = st * jnp.exp(b_end) + lax.dot_general(
                v, ks, (((0,), (0,)), ((), ())), preferred_element_type=F32)
            r = lax.rsqrt(jnp.mean(o * o, axis=-1, keepdims=True) + NORM_EPS)
            o_ref[rs, cs] = (o * r * gn * z_ref[rs, cs].astype(F32)).astype(o_ref.dtype)


def _hgrn_recurrence(qs, f, iv, zs, lower_bounds, g_norm, layer):
    b, s, w = qs.shape
    heads = 2
    cols = heads * HG_DIM
    tt = _pick(s, 256)
    depth = lower_bounds.shape[0]
    kern = functools.partial(_hgrn_kernel, tt=tt, heads=heads, layer=layer)
    blk = pl.BlockSpec((None, tt, cols), lambda bi, h, t: (bi, t, h))
    return pl.pallas_call(
        kern,
        grid=(b, w // cols, s // tt),
        in_specs=[blk, blk, blk, blk,
                  pl.BlockSpec((depth, cols), lambda bi, h, t: (0, h)),
                  pl.BlockSpec((1, HG_DIM), lambda bi, h, t: (0, 0))],
        out_specs=blk,
        out_shape=jax.ShapeDtypeStruct((b, s, w), BF16),
        scratch_shapes=[pltpu.VMEM((heads, HG_DIM, HG_DIM), F32)],
        compiler_params=_params(3),
        name="hgrn_recurrence",
    )(qs, f, iv, zs, lower_bounds, g_norm.reshape(1, HG_DIM))


def _rot_half_cols(w):
    half = w.shape[-1] // 2
    return jnp.concatenate([-w[..., half:], w[..., :half]], axis=-1)


def kernel(x, positions, l0_norm, l0_w_in, l0_q_norm, l0_w_uq, l0_kv_norm, l0_w_ukv, l0_w_o,
           l1_norm, l1_w_in, l1_g_norm, l1_w_o, lower_bounds, final_norm):
    bsz, seq, d = x.shape
    m = bsz * seq
    q_lora = l0_q_norm.shape[0]
    kv_lora = l0_kv_norm.shape[0]
    heads = l0_w_uq.shape[1] // QK_DIM
    width = heads * V_DIM
    x2 = x.reshape(m, d)

    inv_freq = 1.0 / (ROPE_THETA ** (jnp.arange(0, ROPE_DIM, 2, dtype=F32) / ROPE_DIM))
    ang = positions.reshape(m, 1).astype(F32) * inv_freq
    zeros = jnp.zeros((m, ROPE_DIM), F32)
    cos2 = jnp.concatenate([jnp.cos(ang), jnp.cos(ang), zeros], axis=-1)
    sin2 = jnp.concatenate([jnp.sin(ang), jnp.sin(ang), zeros], axis=-1)

    w_kr = l0_w_in[:, q_lora + kv_lora:q_lora + kv_lora + ROPE_DIM]
    w_small = jnp.concatenate(
        [l0_w_in[:, :q_lora + kv_lora], w_kr, _rot_half_cols(w_kr)], axis=-1).astype(BF16)
    w_z0 = l0_w_in[:, q_lora + kv_lora + ROPE_DIM:].astype(BF16)
    wq = l0_w_uq.reshape(q_lora, heads, QK_DIM)
    wq_rot = wq[..., NOPE_DIM:]
    w_q = jnp.concatenate([wq, _rot_half_cols(wq_rot)], axis=-1).reshape(
        q_lora, heads * QK_PAD).astype(BF16)
    w_kv = l0_w_ukv.astype(BF16)

    h0 = _rmsnorm(x2, l0_norm, BF16, "norm0")
    cq, ckv, kr = _mla_down(h0, w_small, l0_q_norm, l0_kv_norm, cos2, sin2, q_lora, kv_lora)
    zg0 = _simple_mm(h0, w_z0, _ep_silu, BF16, "mla_gate")

    hq = 4
    tm = _pick(m, 1024)
    row_tab = pl.BlockSpec((tm, LANES), lambda i, j: (i, 0))
    scale = LOG2E / math.sqrt(QK_DIM)
    q = _matmul(cq, w_q, tm=tm, tn=hq * QK_PAD,
                epilogue=functools.partial(_ep_q_up, heads=hq, scale=scale),
                extras=(cos2, sin2), extra_specs=(row_tab, row_tab),
                out_shape=jax.ShapeDtypeStruct((m, heads * QK_PAD), BF16),
                out_specs=pl.BlockSpec((tm, hq * QK_PAD), lambda i, j: (i, j)), name="mla_q_up")
    kfull, v = _matmul(ckv, w_kv, tm=tm, tn=hq * QK_PAD,
                       epilogue=functools.partial(_ep_kv_up, heads=hq),
                       extras=(kr,), extra_specs=(row_tab,),
                       out_shape=[jax.ShapeDtypeStruct((m, heads * QK_PAD), BF16),
                                  jax.ShapeDtypeStruct((m, width), BF16)],
                       out_specs=[pl.BlockSpec((tm, hq * QK_PAD), lambda i, j: (i, j)),
                                  pl.BlockSpec((tm, hq * V_DIM), lambda i, j: (i, j))],
                       name="mla_kv_up")
    og0 = _attention(q.reshape(bsz, seq, -1), kfull.reshape(bsz, seq, -1),
                     v.reshape(bsz, seq, -1), zg0.reshape(bsz, seq, -1), heads)
    x1 = _simple_mm(og0.reshape(m, width), l0_w_o.astype(BF16), _ep_residual, F32, "mla_out",
                    residual=x2)

    hw = l1_w_o.shape[0]
    w1 = l1_w_in.astype(BF16)
    h1 = _rmsnorm(x1, l1_norm, BF16, "norm1")
    qs = _simple_mm(h1, w1[:, :hw], _ep_silu, BF16, "hgrn_q")
    fr = _simple_mm(h1, w1[:, hw:2 * hw], _ep_cast, F32, "hgrn_f")
    iv = _simple_mm(h1, w1[:, 2 * hw:3 * hw], _ep_cast, BF16, "hgrn_i")
    zs = _simple_mm(h1, w1[:, 3 * hw:], _ep_silu, BF16, "hgrn_z")
    r3 = lambda t: t.reshape(bsz, seq, hw)
    og1 = _hgrn_recurrence(r3(qs), r3(fr), r3(iv), r3(zs), lower_bounds, l1_g_norm, layer=1)
    x2f = _simple_mm(og1.reshape(m, hw), l1_w_o.astype(BF16), _ep_residual, F32, "hgrn_out",
                     residual=x1)

    return _rmsnorm(x2f, final_norm, F32, "final_norm").reshape(bsz, seq, d)
```

```python
import functools
import math

import jax
import jax.numpy as jnp
from jax import lax
from jax.experimental import pallas as pl
from jax.experimental.pallas import tpu as pltpu

NORM_EPS = 1e-6
NOPE_DIM = 128
ROPE_DIM = 64
V_DIM = 128
QK_DIM = NOPE_DIM + ROPE_DIM
QK_PAD = 256
ROPE_THETA = 10000.0
HG_DIM = 128
CHUNK = 64
LANES = 128
VMEM_LIMIT = 52 * 1024 * 1024
LOG2E = 1.4426950408889634

F32 = jnp.float32
BF16 = jnp.bfloat16


def _params(n_axes, vmem=VMEM_LIMIT):
    return pltpu.CompilerParams(
        dimension_semantics=("arbitrary",) * n_axes, vmem_limit_bytes=vmem)


def _sigmoid(x):
    return 1.0 / (1.0 + jnp.exp(-x))


def _silu(x):
    return x * _sigmoid(x)


def _pick(n, pref):
    t = min(n, pref)
    while n % t:
        t //= 2
    return t


def _rmsnorm_kernel(x_ref, w_ref, o_ref):
    x = x_ref[...]
    r = lax.rsqrt(jnp.mean(x * x, axis=-1, keepdims=True) + NORM_EPS)
    o_ref[...] = (x * r * w_ref[...]).astype(o_ref.dtype)


def _rmsnorm(x, w, out_dtype, name):
    m, d = x.shape
    tm = _pick(m, 256)
    return pl.pallas_call(
        _rmsnorm_kernel,
        grid=(m // tm,),
        in_specs=[pl.BlockSpec((tm, d), lambda i: (i, 0)),
                  pl.BlockSpec((1, d), lambda i: (0, 0))],
        out_specs=pl.BlockSpec((tm, d), lambda i: (i, 0)),
        out_shape=jax.ShapeDtypeStruct((m, d), out_dtype),
        compiler_params=_params(1),
        name=name,
    )(x, w.reshape(1, d))


def _mm_kernel(a_ref, b_ref, *refs, epilogue, n_extra):
    acc = jnp.dot(a_ref[...], b_ref[...].astype(BF16), preferred_element_type=F32)
    epilogue(acc, refs[:n_extra], refs[n_extra:])


def _matmul(a, b, *, tm, tn, epilogue, extras=(), extra_specs=(), out_shape, out_specs, name,
            n=None, col0=0):
    m, k = a.shape
    n = b.shape[1] if n is None else n
    tm = _pick(m, tm)
    tn = _pick(n, tn)
    assert col0 % tn == 0
    j0 = col0 // tn
    kern = functools.partial(_mm_kernel, epilogue=epilogue, n_extra=len(extras))
    return pl.pallas_call(
        kern,
        grid=(m // tm, n // tn),
        in_specs=[pl.BlockSpec((tm, k), lambda i, j: (i, 0)),
                  pl.BlockSpec((k, tn), lambda i, j: (0, j + j0))] + list(extra_specs),
        out_specs=out_specs,
        out_shape=out_shape,
        compiler_params=_params(2),
        name=name,
    )(a, b, *extras)


def _ep_silu(acc, extras, outs):
    outs[0][...] = _silu(acc).astype(outs[0].dtype)


def _ep_cast(acc, extras, outs):
    outs[0][...] = acc.astype(outs[0].dtype)


def _ep_residual(acc, extras, outs):
    outs[0][...] = extras[0][...] + acc


def _simple_mm(a, b, epilogue, out_dtype, name, tm=1024, tn=512, residual=None, n=None, col0=0):
    m, _ = a.shape
    n = b.shape[1] if n is None else n
    tm = _pick(m, tm)
    tn = _pick(n, tn)
    tile = pl.BlockSpec((tm, tn), lambda i, j: (i, j))
    extras, extra_specs = ((residual,), (tile,)) if residual is not None else ((), ())
    return _matmul(a, b, tm=tm, tn=tn, epilogue=epilogue, extras=extras, extra_specs=extra_specs,
                   out_shape=jax.ShapeDtypeStruct((m, n), out_dtype), out_specs=tile, name=name,
                   n=n, col0=col0)


def _rope_pair(x, cos2, sin2):
    return x * cos2 + pltpu.roll(x, ROPE_DIM, 1) * sin2


def _mla_down_kernel(a_ref, b_ref, qn_ref, kvn_ref, cos_ref, sin_ref,
                     cq_ref, ckv_ref, kr_ref, acc_ref, *, q_lora, kv_lora):
    kk = pl.program_id(1)
    part = jnp.dot(a_ref[...], b_ref[...], preferred_element_type=F32)

    @pl.when(kk == 0)
    def _():
        acc_ref[...] = part

    @pl.when(kk != 0)
    def _():
        acc_ref[...] += part

    @pl.when(kk == pl.num_programs(1) - 1)
    def _():
        cq = acc_ref[:, :q_lora]
        r = lax.rsqrt(jnp.mean(cq * cq, axis=-1, keepdims=True) + NORM_EPS)
        cq_ref[...] = (cq * r * qn_ref[...]).astype(cq_ref.dtype)
        ckv = acc_ref[:, q_lora:q_lora + kv_lora]
        r = lax.rsqrt(jnp.mean(ckv * ckv, axis=-1, keepdims=True) + NORM_EPS)
        ckv_ref[...] = (ckv * r * kvn_ref[...]).astype(ckv_ref.dtype)
        kr = acc_ref[:, q_lora + kv_lora:]
        kr_ref[...] = _rope_pair(kr, cos_ref[...], sin_ref[...]).astype(kr_ref.dtype)


def _mla_down(h, w_small, q_norm, kv_norm, cos2, sin2, q_lora, kv_lora):
    m, d = h.shape
    n = w_small.shape[1]
    tm = _pick(m, 1024)
    tk = _pick(d, 1024)
    kern = functools.partial(_mla_down_kernel, q_lora=q_lora, kv_lora=kv_lora)
    row = lambda i, k: (i, 0)
    return pl.pallas_call(
        kern,
        grid=(m // tm, d // tk),
        in_specs=[pl.BlockSpec((tm, tk), lambda i, k: (i, k)),
                  pl.BlockSpec((tk, n), lambda i, k: (k, 0)),
                  pl.BlockSpec((1, q_lora), lambda i, k: (0, 0)),
                  pl.BlockSpec((1, kv_lora), lambda i, k: (0, 0)),
                  pl.BlockSpec((tm, LANES), row),
                  pl.BlockSpec((tm, LANES), row)],
        out_specs=[pl.BlockSpec((tm, q_lora), row),
                   pl.BlockSpec((tm, kv_lora), row),
                   pl.BlockSpec((tm, LANES), row)],
        out_shape=[jax.ShapeDtypeStruct((m, q_lora), BF16),
                   jax.ShapeDtypeStruct((m, kv_lora), BF16),
                   jax.ShapeDtypeStruct((m, LANES), BF16)],
        scratch_shapes=[pltpu.VMEM((tm, n), F32)],
        compiler_params=_params(2),
        name="mla_down",
    )(h, w_small, q_norm.reshape(1, -1), kv_norm.reshape(1, -1), cos2, sin2)


def _ep_q_up(acc, extras, outs, *, heads, scale):
    cos2 = extras[0][...]
    sin2 = extras[1][...]
    for j in range(heads):
        base = j * QK_PAD
        outs[0][:, base:base + NOPE_DIM] = (acc[:, base:base + NOPE_DIM] * scale).astype(BF16)
        rot = _rope_pair(acc[:, base + NOPE_DIM:base + QK_PAD], cos2, sin2)
        outs[0][:, base + NOPE_DIM:base + QK_PAD] = (rot * scale).astype(BF16)


def _ep_kv_up(acc, extras, outs, *, heads):
    kr = extras[0][...]
    for j in range(heads):
        base = j * QK_PAD
        outs[0][:, base:base + NOPE_DIM] = acc[:, base:base + NOPE_DIM].astype(BF16)
        outs[0][:, base + NOPE_DIM:base + QK_PAD] = kr
        outs[1][:, j * V_DIM:(j + 1) * V_DIM] = acc[:, base + NOPE_DIM:base + QK_PAD].astype(BF16)


def _attn_kernel(q_ref, k_ref, v_ref, z_ref, o_ref, vx_ref, sa_ref, sb_ref, pa_ref, pb_ref,
                 aa_ref, ab_ref, m_ref, acc_ref, *, tq, tk):
    qi = pl.program_id(2)
    neg = jnp.finfo(F32).min

    @pl.when(qi == 0)
    def _():
        vx_ref[:, :V_DIM] = v_ref[...]
        vx_ref[:, V_DIM:] = jnp.ones((vx_ref.shape[0], V_DIM), BF16)

    upper = slice(0, tk)
    lower = slice(tk, tq)
    every = slice(None)

    def scores(s_ref, kt, rows=every):
        off = pl.multiple_of(kt * tk, tk)
        s_ref[rows, :] = lax.dot_general(q_ref[rows, :], k_ref[pl.ds(off, tk), :],
                                         (((1,), (1,)), ((), ())), preferred_element_type=F32)

    def softmax(s_ref, p_ref, a_ref, rows=every, masked=False, first=False):
        s = s_ref[rows, :]
        if masked:
            row = lax.broadcasted_iota(jnp.int32, s.shape, 0)
            col = lax.broadcasted_iota(jnp.int32, s.shape, 1)
            s = jnp.where(col <= row, s, neg)
        mx = jnp.max(s, axis=1, keepdims=True)
        if first:
            m_new = jnp.broadcast_to(mx, (s.shape[0], LANES))
        else:
            m_old = m_ref[rows, :]
            m_new = jnp.maximum(m_old, mx)
            a_ref[rows, :] = jnp.exp2(m_old - m_new)
        m_ref[rows, :] = m_new
        p_ref[rows, :] = jnp.exp2(s - jnp.concatenate([m_new] * (tk // LANES), axis=1)).astype(BF16)

    def accumulate(p_ref, a_ref, vt, rows=every, first=False):
        off = pl.multiple_of(vt * tk, tk)
        pv = jnp.dot(p_ref[rows, :], vx_ref[pl.ds(off, tk), :], preferred_element_type=F32)
        if first:
            acc_ref[rows, :] = pv
        else:
            a = a_ref[rows, :]
            acc_ref[rows, :] = jnp.concatenate([a, a], axis=1) * acc_ref[rows, :] + pv

    d0 = 2 * qi
    scores(sb_ref, d0 + 1, lower)
    scores(sa_ref, d0)
    softmax(sb_ref, pb_ref, ab_ref, lower, masked=True, first=True)
    accumulate(pb_ref, ab_ref, d0 + 1, lower, first=True)
    scores(sb_ref, 0)
    acc_ref[upper, :] = jnp.zeros((tk, 2 * V_DIM), F32)
    aa_ref[upper, :] = jnp.zeros((tk, LANES), F32)
    softmax(sa_ref, pa_ref, aa_ref, upper, masked=True, first=True)
    softmax(sa_ref, pa_ref, aa_ref, lower)

    def body(i, carry):
        scores(sa_ref, 2 * i + 1)
        accumulate(pa_ref, aa_ref, jnp.where(i == 0, d0, 2 * i - 1))
        softmax(sb_ref, pb_ref, ab_ref)
        scores(sb_ref, 2 * i + 2)
        accumulate(pb_ref, ab_ref, 2 * i)
        softmax(sa_ref, pa_ref, aa_ref)
        return carry

    lax.fori_loop(0, qi, body, 0)
    accumulate(pa_ref, aa_ref, jnp.where(qi == 0, d0, 2 * qi - 1))
    acc = acc_ref[...]
    o = acc[:, :V_DIM] / acc[:, V_DIM:]
    o_ref[...] = (o * z_ref[...].astype(F32)).astype(o_ref.dtype)


def _attention(q, k, v, zg, heads):
    b, s, _ = q.shape
    tq = _pick(s, 1024)
    tk = tq // 2
    kern = functools.partial(_attn_kernel, tq=tq, tk=tk)
    return pl.pallas_call(
        kern,
        grid=(b, heads, s // tq),
        in_specs=[pl.BlockSpec((None, tq, QK_PAD), lambda bi, h, i: (bi, i, h)),
                  pl.BlockSpec((None, s, QK_PAD), lambda bi, h, i: (bi, 0, h)),
                  pl.BlockSpec((None, s, V_DIM), lambda bi, h, i: (bi, 0, h)),
                  pl.BlockSpec((None, tq, V_DIM), lambda bi, h, i: (bi, i, h))],
        out_specs=pl.BlockSpec((None, tq, V_DIM), lambda bi, h, i: (bi, i, h)),
        out_shape=jax.ShapeDtypeStruct((b, s, heads * V_DIM), BF16),
        scratch_shapes=[pltpu.VMEM((s, 2 * V_DIM), BF16),
                        pltpu.VMEM((tq, tk), F32), pltpu.VMEM((tq, tk), F32),
                        pltpu.VMEM((tq, tk), BF16), pltpu.VMEM((tq, tk), BF16),
                        pltpu.VMEM((tq, LANES), F32), pltpu.VMEM((tq, LANES), F32),
                        pltpu.VMEM((tq, LANES), F32), pltpu.VMEM((tq, 2 * V_DIM), F32)],
        compiler_params=_params(3),
        name="mla_attention",
    )(q, k, v, zg)


def _cumsum_rows(x):
    n = x.shape[0]
    row = lax.broadcasted_iota(jnp.int32, x.shape, 0)
    sh = 1
    while sh < n:
        x = x + jnp.where(row >= sh, pltpu.roll(x, sh, 0), 0.0)
        sh *= 2
    return x


def _hgrn_kernel(q_ref, f_ref, i_ref, z_ref, lb_ref, gn_ref, o_ref, st_ref, *, tt, heads, layer):
    @pl.when(pl.program_id(2) == 0)
    def _():
        st_ref[...] = jnp.zeros_like(st_ref)

    lbp = lb_ref[...]
    e = jnp.exp(lbp - jnp.max(lbp, axis=0, keepdims=True))
    soft = e / jnp.sum(e, axis=0, keepdims=True)
    lb_all = jnp.sum(soft[:layer + 1], axis=0, keepdims=True) - soft[0:1]

    row = lax.broadcasted_iota(jnp.int32, (CHUNK, CHUNK), 0)
    col = lax.broadcasted_iota(jnp.int32, (CHUNK, CHUNK), 1)
    causal = col <= row
    gn = gn_ref[...]

    for hh in range(heads):
        cs = slice(hh * HG_DIM, (hh + 1) * HG_DIM)
        lb = lb_all[:, cs]
        st = st_ref[hh]
        for c in range(tt // CHUNK):
            rs = slice(c * CHUNK, (c + 1) * CHUNK)
            v = i_ref[rs, cs]
            forget = lb + (1.0 - lb) / (1.0 + jnp.exp2(f_ref[rs, cs] * (-LOG2E)))
            k = 1.0 - forget
            b = _cumsum_rows(jnp.log2(forget))
            b_mid = b[CHUNK // 2:CHUNK // 2 + 1, :]
            b_end = b[CHUNK - 1:CHUNK, :]
            e = jnp.exp2(b - b_mid)
            qd = q_ref[rs, cs].astype(F32) * e
            kd = k / e
            qs = (qd * jnp.exp2(b_mid)).astype(BF16)
            ks = (kd * jnp.exp2(b_end - b_mid)).astype(BF16)
            att = lax.dot_general(qd.astype(BF16), kd.astype(BF16), (((1,), (1,)), ((), ())),
                                  preferred_element_type=F32)
            att = jnp.where(causal, att, 0.0).astype(BF16)
            o = (jnp.dot(att, v, preferred_element_type=F32)
                 + lax.dot_general(qs, st.astype(BF16), (((1,), (1,)), ((), ())),
                                   preferred_element_type=F32))
            st = st * jnp.exp2(b_end) + lax.dot_general(
                v, ks, (((0,), (0,)), ((), ())), preferred_element_type=F32)
            r = lax.rsqrt(jnp.mean(o * o, axis=-1, keepdims=True) + NORM_EPS)
            o_ref[rs, cs] = (o * r * gn * z_ref[rs, cs].astype(F32)).astype(o_ref.dtype)
        st_ref[hh] = st


def _hgrn_recurrence(qs, f, iv, zs, lower_bounds, g_norm, layer):
    b, s, w = qs.shape
    heads = min(8, w // HG_DIM)
    cols = heads * HG_DIM
    tt = _pick(s, 512)
    depth = lower_bounds.shape[0]
    kern = functools.partial(_hgrn_kernel, tt=tt, heads=heads, layer=layer)
    blk = pl.BlockSpec((None, tt, cols), lambda bi, h, t: (bi, t, h))
    return pl.pallas_call(
        kern,
        grid=(b, w // cols, s // tt),
        in_specs=[blk, blk, blk, blk,
                  pl.BlockSpec((depth, cols), lambda bi, h, t: (0, h)),
                  pl.BlockSpec((1, HG_DIM), lambda bi, h, t: (0, 0))],
        out_specs=blk,
        out_shape=jax.ShapeDtypeStruct((b, s, w), BF16),
        scratch_shapes=[pltpu.VMEM((heads, HG_DIM, HG_DIM), F32)],
        compiler_params=_params(3),
        name="hgrn_recurrence",
    )(qs, f, iv, zs, lower_bounds, g_norm.reshape(1, HG_DIM))


def _rot_half_cols(w):
    half = w.shape[-1] // 2
    return jnp.concatenate([-w[..., half:], w[..., :half]], axis=-1)


def kernel(x, positions, l0_norm, l0_w_in, l0_q_norm, l0_w_uq, l0_kv_norm, l0_w_ukv, l0_w_o,
           l1_norm, l1_w_in, l1_g_norm, l1_w_o, lower_bounds, final_norm):
    bsz, seq, d = x.shape
    m = bsz * seq
    q_lora = l0_q_norm.shape[0]
    kv_lora = l0_kv_norm.shape[0]
    heads = l0_w_uq.shape[1] // QK_DIM
    width = heads * V_DIM
    x2 = x.reshape(m, d)

    inv_freq = 1.0 / (ROPE_THETA ** (jnp.arange(0, ROPE_DIM, 2, dtype=F32) / ROPE_DIM))
    ang = positions.reshape(m, 1).astype(F32) * inv_freq
    zeros = jnp.zeros((m, ROPE_DIM), F32)
    cos2 = jnp.concatenate([jnp.cos(ang), jnp.cos(ang), zeros], axis=-1)
    sin2 = jnp.concatenate([jnp.sin(ang), jnp.sin(ang), zeros], axis=-1)

    w_kr = l0_w_in[:, q_lora + kv_lora:q_lora + kv_lora + ROPE_DIM]
    w_small = jnp.concatenate(
        [l0_w_in[:, :q_lora + kv_lora], w_kr, _rot_half_cols(w_kr)], axis=-1).astype(BF16)
    w_z0 = l0_w_in[:, q_lora + kv_lora + ROPE_DIM:].astype(BF16)
    wq = l0_w_uq.reshape(q_lora, heads, QK_DIM)
    wq_rot = wq[..., NOPE_DIM:]
    w_q = jnp.concatenate([wq, _rot_half_cols(wq_rot)], axis=-1).reshape(
        q_lora, heads * QK_PAD).astype(BF16)

    h0 = _rmsnorm(x2, l0_norm, BF16, "norm0")
    cq, ckv, kr = _mla_down(h0, w_small, l0_q_norm, l0_kv_norm, cos2, sin2, q_lora, kv_lora)
    zg0 = _simple_mm(h0, w_z0, _ep_silu, BF16, "mla_gate")

    hq = 4
    tm = _pick(m, 1024)
    row_tab = pl.BlockSpec((tm, LANES), lambda i, j: (i, 0))
    scale = LOG2E / math.sqrt(QK_DIM)
    q = _matmul(cq, w_q, tm=tm, tn=hq * QK_PAD,
                epilogue=functools.partial(_ep_q_up, heads=hq, scale=scale),
                extras=(cos2, sin2), extra_specs=(row_tab, row_tab),
                out_shape=jax.ShapeDtypeStruct((m, heads * QK_PAD), BF16),
                out_specs=pl.BlockSpec((tm, hq * QK_PAD), lambda i, j: (i, j)), name="mla_q_up")
    kfull, v = _matmul(ckv, l0_w_ukv, tm=tm, tn=hq * QK_PAD,
                       epilogue=functools.partial(_ep_kv_up, heads=hq),
                       extras=(kr,), extra_specs=(row_tab,),
                       out_shape=[jax.ShapeDtypeStruct((m, heads * QK_PAD), BF16),
                                  jax.ShapeDtypeStruct((m, width), BF16)],
                       out_specs=[pl.BlockSpec((tm, hq * QK_PAD), lambda i, j: (i, j)),
                                  pl.BlockSpec((tm, hq * V_DIM), lambda i, j: (i, j))],
                       name="mla_kv_up")
    og0 = _attention(q.reshape(bsz, seq, -1), kfull.reshape(bsz, seq, -1),
                     v.reshape(bsz, seq, -1), zg0.reshape(bsz, seq, -1), heads)
    x1 = _simple_mm(og0.reshape(m, width), l0_w_o, _ep_residual, F32, "mla_out", residual=x2)

    hw = l1_w_o.shape[0]
    h1 = _rmsnorm(x1, l1_norm, BF16, "norm1")
    qs = _simple_mm(h1, l1_w_in, _ep_silu, BF16, "hgrn_q", n=hw, col0=0)
    fr = _simple_mm(h1, l1_w_in, _ep_cast, F32, "hgrn_f", n=hw, col0=hw)
    iv = _simple_mm(h1, l1_w_in, _ep_cast, BF16, "hgrn_i", n=hw, col0=2 * hw)
    zs = _simple_mm(h1, l1_w_in, _ep_silu, BF16, "hgrn_z", n=hw, col0=3 * hw)
    r3 = lambda t: t.reshape(bsz, seq, hw)
    og1 = _hgrn_recurrence(r3(qs), r3(fr), r3(iv), r3(zs), lower_bounds, l1_g_norm, layer=1)
    x2f = _simple_mm(og1.reshape(m, hw), l1_w_o, _ep_residual, F32, "hgrn_out", residual=x1)

    return _rmsnorm(x2f, final_norm, F32, "final_norm").reshape(bsz, seq, d)
```
